```python
import numpy as np
import jax
import jax.numpy as jnp
from jax import lax

D_MODEL = 1024
BATCH = 4
SEQ = 4096
DEPTH = 2
DEC_BATCH = 128
DEC_SEQ = 1
PAST_LEN = 16384
PAGE_SIZE = 128

HEAD_DIM = 64
SB_HEADS = D_MODEL // (4 * HEAD_DIM)
DSA_HEADS = D_MODEL // (4 * HEAD_DIM)
IDX_HEADS = 8
IDX_DIM = 64
INDEX_TOPK = 256
MLA_HEADS = D_MODEL // (2 * HEAD_DIM)
MLA_NOPE = 64
MLA_ROPE = 32
MLA_V = HEAD_DIM
Q_LORA = 256
KV_LORA = 256
MIX_WIDTH = SB_HEADS * HEAD_DIM + DSA_HEADS * HEAD_DIM + MLA_HEADS * MLA_V
ROT_DIM = HEAD_DIM // 4
IDX_ROT = IDX_DIM // 4
ROPE_THETA = 500000.0
QBLOCK = 128
N_EXPERTS = 32
TOP_K = 4
D_FF = D_MODEL
SWIGLU_LIMIT = 7.0
SWIGLU_ALPHA = 1.702
LN_EPS = 1e-5
RMS_EPS = 1e-6
DN_ALPHA = (2 * DEPTH) ** 0.25
DN_BETA = (8 * DEPTH) ** -0.25
IN_WIDTHS = (SB_HEADS * HEAD_DIM, HEAD_DIM, HEAD_DIM,
             DSA_HEADS * HEAD_DIM, HEAD_DIM, HEAD_DIM,
             IDX_HEADS * IDX_DIM, IDX_HEADS, IDX_DIM,
             Q_LORA, KV_LORA, MLA_ROPE)
D_IN = sum(IN_WIDTHS)

kernel_name = 'hybrid_sb_dsa_mla_moe_decoder_step'


def layer_norm(x, g, b):
    xf = x.astype(jnp.float32)
    xc = xf - jnp.mean(xf, axis=-1, keepdims=True)
    var = jnp.mean(xc * xc, axis=-1, keepdims=True)
    return (xc * lax.rsqrt(var + LN_EPS) * g + b).astype(x.dtype)


def rms_norm(x, g):
    xf = x.astype(jnp.float32)
    return (xf * lax.rsqrt(jnp.mean(xf * xf, axis=-1, keepdims=True) + RMS_EPS) * g).astype(x.dtype)


def apply_rope(x, pos, rot_dim):
    half = rot_dim // 2
    inv = ROPE_THETA ** (-jnp.arange(half, dtype=jnp.float32) / half)
    ang = pos.astype(jnp.float32)[:, None] * inv[None, :]
    shp = (pos.shape[0],) + (1,) * (x.ndim - 3) + (half,)
    cos = jnp.cos(ang).reshape(shp).astype(x.dtype)
    sin = jnp.sin(ang).reshape(shp).astype(x.dtype)
    x1, x2, rest = x[..., :half], x[..., half:rot_dim], x[..., rot_dim:]
    return jnp.concatenate([x1 * cos - x2 * sin, x2 * cos + x1 * sin, rest], axis=-1)


def modulation(c, w_ada, b_ada):
    m = jax.nn.silu(c) @ w_ada + b_ada
    return jnp.split(m[:, None, :], 6, axis=-1)


def mixer_inputs(h, pos, w_in, g_qn, g_kvn, w_uq):
    n, t, _ = h.shape
    parts = jnp.split(h @ w_in, np.cumsum(IN_WIDTHS)[:-1].tolist(), axis=-1)
    q_sb, k_sb, v_sb, q_ds, k_ds, v_ds, q_ix, w_ix, k_ix, c_q, ckv, k_r = parts
    q_sb = q_sb.reshape(n, t, SB_HEADS, HEAD_DIM)
    q_ds = apply_rope(q_ds.reshape(n, t, DSA_HEADS, HEAD_DIM), pos, ROT_DIM)
    k_ds = apply_rope(k_ds, pos, ROT_DIM)
    q_ix = apply_rope(q_ix.reshape(n, t, IDX_HEADS, IDX_DIM), pos, IDX_ROT)
    k_ix = apply_rope(k_ix, pos, IDX_ROT)
    w_ix = w_ix * IDX_HEADS ** -0.5
    q = (rms_norm(c_q, g_qn) @ w_uq).reshape(n, t, MLA_HEADS, MLA_NOPE + MLA_ROPE)
    q_nope = q[..., :MLA_NOPE]
    q_rope = apply_rope(q[..., MLA_NOPE:], pos, MLA_ROPE)
    ckv = rms_norm(ckv, g_kvn)
    k_r = apply_rope(k_r, pos, MLA_ROPE)
    return (q_sb, k_sb, v_sb, q_ds, k_ds, v_ds, q_ix, w_ix, k_ix, q_nope, q_rope, ckv, k_r)


def sb_attend(q, k, v, qpos, kpos):
    z = jnp.einsum('nqhd,nkd->nhqk', q, k).astype(jnp.float32) * HEAD_DIM ** -0.5
    allowed = kpos[None, :] < qpos[:, None]
    log_1m = jnp.where(allowed, jax.nn.log_sigmoid(-z), 0.0)
    tail = lax.cumsum(log_1m, axis=3, reverse=True) - log_1m
    a = jnp.where(allowed, jnp.exp(jax.nn.log_sigmoid(z) + tail), 0.0)
    return jnp.einsum('nhqk,nkd->nqhd', a.astype(v.dtype), v)


def dsa_select(q_ix, w_ix, k_ix, qpos, kpos, k_keep):
    s = jax.nn.relu(jnp.einsum('nqhd,nkd->nqhk', q_ix, k_ix).astype(jnp.float32) * IDX_DIM ** -0.5)
    score = jnp.einsum('nqh,nqhk->nqk', w_ix.astype(jnp.float32), s)
    score = jnp.where(kpos[None, :] <= qpos[:, None], score, -jnp.inf)
    _, idx = lax.top_k(score, k_keep)
    valid = jnp.take(kpos, idx) <= qpos[None, :, None]
    return idx, valid


def dsa_attend(q, k_sel, v_sel, valid):
    logits = jnp.einsum('nqhd,nqkd->nqhk', q, k_sel).astype(jnp.float32) * HEAD_DIM ** -0.5
    p = jax.nn.softmax(jnp.where(valid[:, :, None, :], logits, -jnp.inf), axis=-1)
    return jnp.einsum('nqhk,nqkd->nqhd', p.astype(v_sel.dtype), v_sel)


def causal_softmax(logits, qpos, kpos):
    logits = jnp.where(kpos[None, :] <= qpos[:, None], logits.astype(jnp.float32), -jnp.inf)
    return jax.nn.softmax(logits, axis=-1)


def mla_attend_expanded(q_nope, q_rope, k_nope, k_rope, v, qpos, kpos):
    logits = (jnp.einsum('nqhd,nkhd->nhqk', q_nope, k_nope)
              + jnp.einsum('nqhe,nke->nhqk', q_rope, k_rope)) * (MLA_NOPE + MLA_ROPE) ** -0.5
    p = causal_softmax(logits, qpos, kpos).astype(v.dtype)
    return jnp.einsum('nhqk,nkhd->nqhd', p, v)


def mla_attend_absorbed(q_lat, q_rope, ckv, k_rope, w_uv, qpos, kpos):
    logits = (jnp.einsum('nqhr,nkr->nhqk', q_lat, ckv)
              + jnp.einsum('nqhe,nke->nhqk', q_rope, k_rope)) * (MLA_NOPE + MLA_ROPE) ** -0.5
    p = causal_softmax(logits, qpos, kpos).astype(ckv.dtype)
    o_lat = jnp.einsum('nhqk,nkr->nqhr', p, ckv)
    return jnp.einsum('nqhr,rhd->nqhd', o_lat, w_uv)


def gather_rows(arr, idx):
    return jax.vmap(lambda a, i: a[i])(arr, idx)


def gather_pages(pool, layer, page_table):
    rows = pool[layer, page_table]
    return rows.reshape(page_table.shape[0], -1, pool.shape[-1])


def gather_paged_rows(pool, layer, page_table, new_rows, idx):
    n_past = page_table.shape[1] * PAGE_SIZE
    past_idx = jnp.minimum(idx, n_past - 1)
    lpage = (past_idx // PAGE_SIZE).reshape(idx.shape[0], -1)
    phys = jnp.take_along_axis(page_table, lpage, axis=1).reshape(idx.shape)
    past_rows = pool[layer, phys, past_idx % PAGE_SIZE]
    new_sel = gather_rows(new_rows, jnp.clip(idx - n_past, 0, new_rows.shape[1] - 1))
    return jnp.where((idx < n_past)[..., None], past_rows, new_sel)


def merge_heads(o_sb, o_ds, o_mla):
    n, t = o_sb.shape[:2]
    return jnp.concatenate([o_sb.reshape(n, t, -1), o_ds.reshape(n, t, -1), o_mla.reshape(n, t, -1)], axis=-1)


def moe(h, layer, w_router, b_router, w_gate_up, b_gate_up, w_down, b_down):
    n, t, d = h.shape
    xt = h.reshape(n * t, d)
    logits = (xt @ w_router[layer] + b_router[layer]).astype(jnp.float32)
    top_val, top_idx = lax.top_k(logits, TOP_K)
    gate = jax.nn.softmax(top_val, axis=-1)
    combine = jnp.einsum('tk,tke->te', gate, jax.nn.one_hot(top_idx, N_EXPERTS, dtype=jnp.float32)).astype(xt.dtype)
    out = jnp.zeros_like(xt)
    for e in range(N_EXPERTS):
        gu = xt @ w_gate_up[layer, e] + b_gate_up[layer, e]
        g = jnp.minimum(gu[:, :D_FF], SWIGLU_LIMIT)
        u = jnp.clip(gu[:, D_FF:], -SWIGLU_LIMIT, SWIGLU_LIMIT)
        act = (u + 1.0) * g * jax.nn.sigmoid(SWIGLU_ALPHA * g)
        out = out + combine[:, e:e + 1] * (act @ w_down[layer, e] + b_down[layer, e])
    return out.reshape(n, t, d)


def setup_inputs(seed: int = 0) -> dict:
    key = jax.random.key(seed)
    ks = iter(jax.random.split(key, 40))

    def nrm(shape, scale):
        return jax.random.normal(next(ks), shape, jnp.float32) * scale

    n_pages = PAST_LEN // PAGE_SIZE
    n_used = DEC_BATCH * n_pages
    n_phys = n_used + (n_used + 3) // 4
    page_table = jax.random.permutation(next(ks), n_phys)[:n_used].reshape(DEC_BATCH, n_pages).astype(jnp.int32)

    def pool(d):
        return nrm((DEPTH, n_phys, PAGE_SIZE, d), 1.0)

    return {
        'x_prompt': nrm((BATCH, SEQ, D_MODEL), 1.0),
        'x_sample': nrm((DEC_BATCH, DEC_SEQ, D_MODEL), 1.0),
        'cache_sb_k': pool(HEAD_DIM),
        'cache_sb_v': pool(HEAD_DIM),
        'cache_dsa_k': pool(HEAD_DIM),
        'cache_dsa_v': pool(HEAD_DIM),
        'cache_dsa_idx_k': pool(IDX_DIM),
        'cache_mla_ckv': pool(KV_LORA),
        'cache_mla_krope': pool(MLA_ROPE),
        'page_table': page_table,
        'c_prompt': nrm((BATCH, D_MODEL), 1.0),
        'c_sample': nrm((DEC_BATCH, D_MODEL), 1.0),
        'w_ada': nrm((DEPTH, D_MODEL, 6 * D_MODEL), 0.5 * D_MODEL ** -0.5),
        'b_ada': nrm((DEPTH, 6 * D_MODEL), 0.02),
        'w_in': nrm((DEPTH, D_MODEL, D_IN), D_MODEL ** -0.5),
        'g_q_norm': 1.0 + nrm((DEPTH, Q_LORA), 0.01),
        'g_kv_norm': 1.0 + nrm((DEPTH, KV_LORA), 0.01),
        'w_uq': nrm((DEPTH, Q_LORA, MLA_HEADS * (MLA_NOPE + MLA_ROPE)), Q_LORA ** -0.5),
        'w_uk': nrm((DEPTH, KV_LORA, MLA_HEADS, MLA_NOPE), KV_LORA ** -0.5),
        'w_uv': nrm((DEPTH, KV_LORA, MLA_HEADS, MLA_V), KV_LORA ** -0.5),
        'w_out': nrm((DEPTH, MIX_WIDTH, D_MODEL), DN_BETA * MIX_WIDTH ** -0.5),
        'ln1_g': 1.0 + nrm((DEPTH, D_MODEL), 0.01),
        'ln1_b': nrm((DEPTH, D_MODEL), 0.01),
        'ln2_g': 1.0 + nrm((DEPTH, D_MODEL), 0.01),
        'ln2_b': nrm((DEPTH, D_MODEL), 0.01),
        'w_router': nrm((DEPTH, D_MODEL, N_EXPERTS), D_MODEL ** -0.5),
        'b_router': nrm((DEPTH, N_EXPERTS), 0.01),
        'w_gate_up': nrm((DEPTH, N_EXPERTS, D_MODEL, 2 * D_FF), D_MODEL ** -0.5),
        'b_gate_up': nrm((DEPTH, N_EXPERTS, 2 * D_FF), 0.01),
        'w_down': nrm((DEPTH, N_EXPERTS, D_FF, D_MODEL), DN_BETA * D_FF ** -0.5),
        'b_down': nrm((DEPTH, N_EXPERTS, D_MODEL), 0.01),
    }


def reference(x_prompt, x_sample, cache_sb_k, cache_sb_v, cache_dsa_k, cache_dsa_v, cache_dsa_idx_k,
              cache_mla_ckv, cache_mla_krope, page_table, c_prompt, c_sample,
              w_ada, b_ada, w_in, g_q_norm, g_kv_norm, w_uq, w_uk, w_uv, w_out,
              ln1_g, ln1_b, ln2_g, ln2_b, w_router, b_router, w_gate_up, b_gate_up, w_down, b_down):

    def prompt_mix(h, l):
        n = h.shape[0]
        pos = jnp.arange(SEQ, dtype=jnp.int32)
        (q_sb, k_sb, v_sb, q_ds, k_ds, v_ds, q_ix, w_ix, k_ix,
         q_nope, q_rope, ckv, k_r) = mixer_inputs(h, pos, w_in[l], g_q_norm[l], g_kv_norm[l], w_uq[l])
        k_nope = jnp.einsum('nkr,rhd->nkhd', ckv, w_uk[l])
        v_mla = jnp.einsum('nkr,rhd->nkhd', ckv, w_uv[l])
        k_keep = min(INDEX_TOPK, SEQ // 4)

        def block(i):
            q0 = i * QBLOCK
            qpos = q0 + jnp.arange(QBLOCK, dtype=jnp.int32)
            sl = lambda a: lax.dynamic_slice_in_dim(a, q0, QBLOCK, axis=1)
            o_sb = sb_attend(sl(q_sb), k_sb, v_sb, qpos, pos)
            idx, valid = dsa_select(sl(q_ix), sl(w_ix), k_ix, qpos, pos, k_keep)
            o_ds = dsa_attend(sl(q_ds), gather_rows(k_ds, idx), gather_rows(v_ds, idx), valid)
            o_mla = mla_attend_expanded(sl(q_nope), sl(q_rope), k_nope, k_r, v_mla, qpos, pos)
            return merge_heads(o_sb, o_ds, o_mla)

        out = lax.map(block, jnp.arange(SEQ // QBLOCK))
        out = jnp.moveaxis(out, 0, 1).reshape(n, SEQ, MIX_WIDTH)
        return out, (k_sb, v_sb, k_ds, v_ds, k_ix, ckv, k_r)

    def sample_mix(h, l):
        pos = PAST_LEN + jnp.arange(DEC_SEQ, dtype=jnp.int32)
        kpos = jnp.arange(PAST_LEN + DEC_SEQ, dtype=jnp.int32)
        (q_sb, k_sb, v_sb, q_ds, k_ds, v_ds, q_ix, w_ix, k_ix,
         q_nope, q_rope, ckv, k_r) = mixer_inputs(h, pos, w_in[l], g_q_norm[l], g_kv_norm[l], w_uq[l])
        past = lambda pool_arr: gather_pages(pool_arr, l, page_table)
        k_sb_all = jnp.concatenate([past(cache_sb_k), k_sb], axis=1)
        v_sb_all = jnp.concatenate([past(cache_sb_v), v_sb], axis=1)
        o_sb = sb_attend(q_sb, k_sb_all, v_sb_all, pos, kpos)
        k_ix_all = jnp.concatenate([past(cache_dsa_idx_k), k_ix], axis=1)
        idx, valid = dsa_select(q_ix, w_ix, k_ix_all, pos, kpos, min(INDEX_TOPK, (PAST_LEN + DEC_SEQ) // 4))
        k_sel = gather_paged_rows(cache_dsa_k, l, page_table, k_ds, idx)
        v_sel = gather_paged_rows(cache_dsa_v, l, page_table, v_ds, idx)
        o_ds = dsa_attend(q_ds, k_sel, v_sel, valid)
        ckv_all = jnp.concatenate([past(cache_mla_ckv), ckv], axis=1)
        kr_all = jnp.concatenate([past(cache_mla_krope), k_r], axis=1)
        q_lat = jnp.einsum('nqhd,rhd->nqhr', q_nope, w_uk[l])
        o_mla = mla_attend_absorbed(q_lat, q_rope, ckv_all, kr_all, w_uv[l], pos, kpos)
        return merge_heads(o_sb, o_ds, o_mla), (k_sb, v_sb, k_ds, v_ds, k_ix, ckv, k_r)

    def trunk_layer(x, c, l, mix_fn):
        sh_a, sc_a, g_a, sh_f, sc_f, g_f = modulation(c, w_ada[l], b_ada[l])
        mix, rows = mix_fn(x * (1.0 + sc_a) + sh_a, l)
        x = layer_norm(DN_ALPHA * x + g_a * (mix @ w_out[l]), ln1_g[l], ln1_b[l])
        ffn = moe(x * (1.0 + sc_f) + sh_f, l, w_router, b_router, w_gate_up, b_gate_up, w_down, b_down)
        x = layer_norm(DN_ALPHA * x + g_f * ffn, ln2_g[l], ln2_b[l])
        return x, rows

    y_p, y_s = x_prompt, x_sample
    rows_p, rows_s = [], []
    for l in range(DEPTH):
        y_p, r_p = trunk_layer(y_p, c_prompt, l, prompt_mix)
        rows_p.append(r_p)
        y_s, r_s = trunk_layer(y_s, c_sample, l, sample_mix)
        rows_s.append(r_s)

    def stk(rows, j):
        return jnp.stack([r[j] for r in rows], axis=0)

    return (y_p, y_s,
            stk(rows_p, 0), stk(rows_p, 1), stk(rows_p, 2), stk(rows_p, 3), stk(rows_p, 4), stk(rows_p, 5), stk(rows_p, 6),
            stk(rows_s, 0), stk(rows_s, 1), stk(rows_s, 2), stk(rows_s, 3), stk(rows_s, 4), stk(rows_s, 5), stk(rows_s, 6))
```

```python
import functools
import math

import numpy as np
import jax
import jax.numpy as jnp
from jax import lax
from jax.experimental import pallas as pl
from jax.experimental.pallas import tpu as pltpu

F32 = jnp.float32
BF16 = jnp.bfloat16
I32 = jnp.int32

D_MODEL = 1024
HEAD_DIM = 64
SB_HEADS = 4
DSA_HEADS = 4
IDX_HEADS = 8
IDX_DIM = 64
INDEX_TOPK = 256
MLA_HEADS = 8
MLA_NOPE = 64
MLA_ROPE = 32
Q_LORA = 256
KV_LORA = 256
ROT_DIM = HEAD_DIM // 4
ROPE_THETA = 500000.0
N_EXPERTS = 32
TOP_K = 4
D_FF = D_MODEL
SWIGLU_LIMIT = 7.0
SWIGLU_ALPHA = 1.702
LN_EPS = 1e-5
RMS_EPS = 1e-6
PAGE_SIZE = 128
IN_WIDTHS = (SB_HEADS * HEAD_DIM, HEAD_DIM, HEAD_DIM,
             DSA_HEADS * HEAD_DIM, HEAD_DIM, HEAD_DIM,
             IDX_HEADS * IDX_DIM, IDX_HEADS, IDX_DIM,
             Q_LORA, KV_LORA, MLA_ROPE)
MLA_SCALE = (MLA_NOPE + MLA_ROPE) ** -0.5
NEG_INF = float("-inf")
SB_DEAD = -104.0

LANES = 128
VMEM_LIMIT = 56 * 1024 * 1024

C_QSB = 0
C_QDS = C_QSB + SB_HEADS * LANES
C_QIX = C_QDS + DSA_HEADS * LANES
C_CQ = C_QIX + IDX_HEADS * LANES
C_CKV = C_CQ + Q_LORA
C_KSV = C_CKV + KV_LORA
C_KDV = C_KSV + LANES
C_KXR = C_KDV + LANES
N_PROJ = C_KXR + LANES
KR_OFF = IDX_DIM
WIX_OFF = IDX_DIM + MLA_ROPE


def _dot(a, b):
    return jnp.dot(a, b, preferred_element_type=F32)


def _dot_nt(a, b):
    return lax.dot_general(a, b, (((1,), (1,)), ((), ())), preferred_element_type=F32)


def _cparams(sem):
    return pltpu.CompilerParams(dimension_semantics=sem, vmem_limit_bytes=VMEM_LIMIT)


def _prep_w_in(w_in_l):
    offs = np.cumsum((0,) + IN_WIDTHS)
    sec = [w_in_l[:, offs[i]:offs[i + 1]] for i in range(len(IN_WIDTHS))]
    q_sb, k_sb, v_sb, q_ds, k_ds, v_ds, q_ix, w_ix, k_ix, c_q, ckv, k_r = sec
    d = w_in_l.shape[0]

    def heads_padded(w, nh, scale):
        w = (w * scale).reshape(d, nh, HEAD_DIM)
        return jnp.concatenate([w, jnp.zeros_like(w)], axis=-1).reshape(d, nh * LANES)

    cols = [
        heads_padded(q_sb, SB_HEADS, HEAD_DIM ** -0.5),
        heads_padded(q_ds, DSA_HEADS, HEAD_DIM ** -0.5),
        heads_padded(q_ix, IDX_HEADS, IDX_DIM ** -0.5),
        c_q, ckv,
        k_sb, v_sb, k_ds, v_ds,
        k_ix, k_r, w_ix, jnp.zeros((d, LANES - IDX_DIM - MLA_ROPE - IDX_HEADS), w_in_l.dtype),
    ]
    return jnp.concatenate(cols, axis=1).astype(BF16)


def _prep_w_uq(w_uq_l):
    r = w_uq_l.shape[0]
    w = w_uq_l.reshape(r, MLA_HEADS, MLA_NOPE + MLA_ROPE)
    nope, rope = w[..., :MLA_NOPE], w[..., MLA_NOPE:]
    z64 = jnp.zeros((r, MLA_HEADS, HEAD_DIM), w.dtype)
    even = (jnp.arange(MLA_HEADS) % 2 == 0)[None, :, None]
    nope_c = jnp.where(even, jnp.concatenate([nope, z64], -1), jnp.concatenate([z64, nope], -1))
    rope_c = jnp.concatenate([jnp.zeros((r, MLA_HEADS, KR_OFF), w.dtype), rope,
                              jnp.zeros((r, MLA_HEADS, LANES - KR_OFF - MLA_ROPE), w.dtype)], -1)
    return jnp.concatenate([nope_c.reshape(r, -1), rope_c.reshape(r, -1)], axis=1).astype(BF16)


def _rope_tables(pos):
    posf = pos.astype(F32)[:, None]

    def cs(half):
        inv = ROPE_THETA ** (-jnp.arange(half, dtype=F32) / half)
        ang = posf * inv[None, :]
        return jnp.cos(ang), jnp.sin(ang)

    t = pos.shape[0]
    c8, s8 = cs(ROT_DIM // 2)
    c16, s16 = cs(MLA_ROPE // 2)
    one = jnp.ones((t, LANES), F32)
    zero = jnp.zeros((t, LANES), F32)

    def put(base, off, val):
        return base.at[:, off:off + val.shape[1]].set(val)

    h8, h16 = ROT_DIM // 2, MLA_ROPE // 2
    a_c = put(put(one, 0, c8), h8, c8)
    a_s1 = put(zero, 0, -s8)
    a_s2 = put(zero, h8, s8)
    r_c = put(put(one, KR_OFF, c16), KR_OFF + h16, c16)
    r_s1 = put(zero, KR_OFF, -s16)
    r_s2 = put(zero, KR_OFF + h16, s16)
    b_c = put(put(r_c, 0, c8), h8, c8)
    return jnp.stack([a_c, a_s1, a_s2, b_c, r_s1, r_s2, r_c, zero, zero], axis=0)


def _rope_a(x, tab):
    w = x.shape[-1]
    reps = w // LANES
    h = ROT_DIM // 2

    def rep(t):
        return t if reps == 1 else jnp.concatenate([t] * reps, axis=1)

    xa = pltpu.roll(x, w - h, axis=1)
    xb = pltpu.roll(x, h, axis=1)
    return x * rep(tab[0]) + xa * rep(tab[1]) + xb * rep(tab[2])


def _rope_r(x, tab):
    w = x.shape[-1]
    reps = w // LANES
    h = MLA_ROPE // 2

    def rep(t):
        return t if reps == 1 else jnp.concatenate([t] * reps, axis=1)

    xa = pltpu.roll(x, w - h, axis=1)
    xb = pltpu.roll(x, h, axis=1)
    return x * rep(tab[6]) + xa * rep(tab[4]) + xb * rep(tab[5])


def _rope_b(x, tab):
    w = x.shape[-1]
    h8, h16 = ROT_DIM // 2, MLA_ROPE // 2
    return (x * tab[3]
            + pltpu.roll(x, w - h8, axis=1) * tab[1] + pltpu.roll(x, h8, axis=1) * tab[2]
            + pltpu.roll(x, w - h16, axis=1) * tab[4] + pltpu.roll(x, h16, axis=1) * tab[5])


def _rms(x, g):
    return x * lax.rsqrt(jnp.mean(x * x, axis=-1, keepdims=True) + RMS_EPS) * g


def _mod_kernel(c_ref, w_ref, b_ref, o_ref):
    c = c_ref[...]
    s = c * (1.0 / (1.0 + jnp.exp(-c)))
    o_ref[...] = _dot(s.astype(BF16), w_ref[...].astype(BF16)) + b_ref[...]


def _modulation(c_all, w_ada, b_ada):
    depth, d, n = w_ada.shape
    m = c_all.shape[0]
    tn = 1536
    return pl.pallas_call(
        _mod_kernel,
        out_shape=jax.ShapeDtypeStruct((depth, m, n), F32),
        grid=(depth, n // tn),
        in_specs=[
            pl.BlockSpec((m, d), lambda l, j: (0, 0)),
            pl.BlockSpec((None, d, tn), lambda l, j: (l, 0, j)),
            pl.BlockSpec((None, 1, tn), lambda l, j: (l, 0, j)),
        ],
        out_specs=pl.BlockSpec((None, m, tn), lambda l, j: (l, 0, j)),
        compiler_params=_cparams(("arbitrary", "arbitrary")),
        name="modulation",
    )(c_all, w_ada, b_ada.reshape(depth, 1, n))


def _in_proj_kernel(is_prompt, x_ref, mod_ref, tab_ref, w_ref, gq_ref, gkv_ref, wuq_ref, wa_ref, wb_ref,
                    *outs):
    (ksb_o, vsb_o, kds_o, vds_o, kix_o, ckv_o, kr_o,
     qsb_o, qds_o, qix_o, wix_o, qa_o, qrp_o, ksv_o, kdv_o, kxr_o, *rest) = outs
    tab = [tab_ref[j] for j in range(9)]
    h = x_ref[...] * (1.0 + mod_ref[1]) + mod_ref[0]
    p = _dot(h.astype(BF16), w_ref[...])

    q_sb = p[:, C_QSB:C_QDS]
    q_ds = _rope_a(p[:, C_QDS:C_QIX], tab)
    q_ix = _rope_a(p[:, C_QIX:C_CQ], tab)
    for hh in range(SB_HEADS):
        qsb_o[hh] = q_sb[:, hh * LANES:(hh + 1) * LANES].astype(qsb_o.dtype)
        qds_o[hh] = q_ds[:, hh * LANES:(hh + 1) * LANES].astype(qds_o.dtype)
    for hh in range(IDX_HEADS):
        qix_o[hh] = q_ix[:, hh * LANES:(hh + 1) * LANES].astype(qix_o.dtype)

    ksv = p[:, C_KSV:C_KDV]
    kdv = _rope_a(p[:, C_KDV:C_KXR], tab)
    lane = lax.broadcasted_iota(I32, kdv.shape, 1)
    kxr = _rope_b(p[:, C_KXR:N_PROJ], tab)
    ksb_o[...] = ksv[:, :HEAD_DIM]
    vsb_o[...] = ksv[:, HEAD_DIM:]
    kds_o[...] = kdv[:, :HEAD_DIM]
    vds_o[...] = kdv[:, HEAD_DIM:]
    kix_o[...] = kxr[:, :IDX_DIM]
    kr_o[...] = kxr[:, KR_OFF:KR_OFF + MLA_ROPE]
    wix_o[...] = kxr * (IDX_HEADS ** -0.5)
    ksv_o[...] = ksv.astype(ksv_o.dtype)
    kdv_o[...] = kdv.astype(kdv_o.dtype)
    kxr_o[...] = jnp.where(lane < WIX_OFF, kxr, 0.0).astype(kxr_o.dtype)

    cq = _rms(p[:, C_CQ:C_CKV], gq_ref[...])
    q = _dot(cq.astype(BF16), wuq_ref[...])
    nq = MLA_HEADS * LANES
    q_rp = _rope_r(q[:, nq:], tab)
    ckv = _rms(p[:, C_CKV:C_KSV], gkv_ref[...])
    ckv_o[...] = ckv
    for hh in range(MLA_HEADS):
        qrp_o[hh] = q_rp[:, hh * LANES:(hh + 1) * LANES].astype(qrp_o.dtype)
    if is_prompt:
        knp_o, vml_o = rest
        ckv_b = ckv.astype(BF16)
        kn = _dot(ckv_b, wa_ref[...])
        vm = _dot(ckv_b, wb_ref[...])
        for pp in range(MLA_HEADS // 2):
            knp_o[pp] = kn[:, pp * LANES:(pp + 1) * LANES].astype(knp_o.dtype)
            vml_o[pp] = vm[:, pp * LANES:(pp + 1) * LANES].astype(vml_o.dtype)
        for hh in range(MLA_HEADS):
            qa_o[hh] = q[:, hh * LANES:(hh + 1) * LANES].astype(qa_o.dtype)
    else:
        for hh in range(MLA_HEADS):
            qn = q[:, hh * LANES:(hh + 1) * LANES].astype(BF16)
            qa_o[hh] = _dot(qn, wa_ref[hh]).astype(qa_o.dtype)


def _in_proj(is_prompt, x2d, mod, tab, w_in_p, g_q, g_kv, w_uq_p, w_a, w_b, tm, qdtype):
    t, d = x2d.shape
    nt = t // tm
    if is_prompt:
        tiles_per_b = nt // mod.shape[0]
        mod_spec = pl.BlockSpec((None, 6, 1, d), lambda i: (i // tiles_per_b, 0, 0, 0))
        tab_spec = pl.BlockSpec((9, tm, LANES), lambda i: (0, i % tiles_per_b, 0))
        wa_spec = pl.BlockSpec(w_a.shape, lambda i: (0, 0))
        wb_spec = pl.BlockSpec(w_b.shape, lambda i: (0, 0))
    else:
        mod_spec = pl.BlockSpec((6, tm, d), lambda i: (0, i, 0))
        tab_spec = pl.BlockSpec((9, 1, LANES), lambda i: (0, 0, 0))
        wa_spec = pl.BlockSpec(w_a.shape, lambda i: (0, 0, 0))
        wb_spec = pl.BlockSpec(w_b.shape, lambda i: (0, 0))

    def rows(w, dt=F32):
        return jax.ShapeDtypeStruct((t, w), dt), pl.BlockSpec((tm, w), lambda i: (i, 0))

    def heads(nh, w, dt):
        return jax.ShapeDtypeStruct((nh, t, w), dt), pl.BlockSpec((nh, tm, w), lambda i: (0, i, 0))

    outs = [rows(HEAD_DIM), rows(HEAD_DIM), rows(HEAD_DIM), rows(HEAD_DIM), rows(IDX_DIM),
            rows(KV_LORA), rows(MLA_ROPE),
            heads(SB_HEADS, LANES, qdtype), heads(DSA_HEADS, LANES, qdtype), heads(IDX_HEADS, LANES, qdtype),
            rows(LANES),
            heads(MLA_HEADS, LANES if is_prompt else KV_LORA, qdtype), heads(MLA_HEADS, LANES, qdtype),
            rows(LANES, qdtype), rows(LANES, qdtype), rows(LANES, qdtype)]
    if is_prompt:
        outs += [heads(MLA_HEADS // 2, LANES, BF16), heads(MLA_HEADS // 2, LANES, BF16)]
    out_shape = [o[0] for o in outs]
    out_specs = [o[1] for o in outs]
    return pl.pallas_call(
        functools.partial(_in_proj_kernel, is_prompt),
        out_shape=out_shape,
        grid=(nt,),
        in_specs=[
            pl.BlockSpec((tm, d), lambda i: (i, 0)),
            mod_spec, tab_spec,
            pl.BlockSpec(w_in_p.shape, lambda i: (0, 0)),
            pl.BlockSpec((1, Q_LORA), lambda i: (0, 0)),
            pl.BlockSpec((1, KV_LORA), lambda i: (0, 0)),
            pl.BlockSpec(w_uq_p.shape, lambda i: (0, 0)),
            wa_spec, wb_spec,
        ],
        out_specs=out_specs,
        compiler_params=_cparams(("arbitrary",)),
        name="in_proj_prompt" if is_prompt else "in_proj_sample",
    )(x2d, mod, tab, w_in_p, g_q, g_kv, w_uq_p, w_a, w_b)


def _sortable(score):
    u = lax.bitcast_convert_type(score, I32)
    u = jnp.where(u < 0, u ^ jnp.int32(0x7FFFFFFF), u)
    return jnp.where(score == 0.0, jnp.int32(0), u)


def _kth_largest(count_ge, kk, shape):
    c0 = count_ge(jnp.zeros(shape, I32))
    base = jnp.where(c0 >= kk, jnp.int32(0), jnp.int32(-2 ** 31))

    def bit_step(b, base):
        cand = base | jnp.left_shift(jnp.int32(1), jnp.int32(30) - b)
        return jnp.where(count_ge(cand) >= kk, cand, base)

    return lax.fori_loop(0, 31, bit_step, base)


def _prompt_attn_kernel(tq, kkeep, qsb, qds, qix, wix, qnp, qrp, ksv, kdv, kxr, knp, vml,
                        out_ref, u_scr):
    i = pl.program_id(1)
    tk = tq
    row = lax.broadcasted_iota(I32, (tq, tk), 0)
    col = lax.broadcasted_iota(I32, (tq, tk), 1)
    tri_gt = jnp.where(row > col, 1.0, 0.0).astype(BF16)
    tri_le = jnp.where(row <= col, 1.0, 0.0).astype(BF16)
    lane = lax.broadcasted_iota(I32, (tq, LANES), 1)

    def blk(j):
        return pl.ds(pl.multiple_of(j * tk, tk), tk)

    def stacked_mask(nh, strict):
        r = lax.broadcasted_iota(I32, (nh * tq, tk), 0) & (tq - 1)
        c = lax.broadcasted_iota(I32, (nh * tq, tk), 1)
        return (c < r) if strict else (c <= r)

    def pair_out(o_even, o_odd, upper):
        if upper:
            return jnp.where(lane < HEAD_DIM, pltpu.roll(o_even, HEAD_DIM, axis=1), o_odd)
        return jnp.where(lane < HEAD_DIM, o_even, pltpu.roll(o_odd, HEAD_DIM, axis=1))

    q_s = qsb[...].reshape(SB_HEADS * tq, LANES)

    def sb_step(j, tail, acc, diag):
        kv = ksv[blk(j), :]
        z = _dot_nt(q_s, kv)
        lg = jnp.log1p(jnp.exp(-jnp.abs(z)))
        lsig = jnp.minimum(z, 0.0) - lg
        l1m = -jnp.maximum(z, 0.0) - lg
        if diag:
            allow = stacked_mask(SB_HEADS, True)
            l1m = jnp.where(allow, l1m, 0.0)
        hi = l1m.astype(BF16)
        lo = (l1m - hi.astype(F32)).astype(BF16)
        tin = _dot(hi, tri_gt) + _dot(lo, tri_gt)
        e = jnp.exp(lsig + tin + tail)
        if diag:
            e = jnp.where(allow, e, 0.0)
        acc = acc + _dot(e.astype(BF16), kv)
        tail = tail + jnp.sum(l1m, axis=1, keepdims=True)
        return tail, acc

    tail0 = jnp.zeros((SB_HEADS * tq, 1), F32)
    acc0 = jnp.zeros((SB_HEADS * tq, LANES), F32)
    tail, acc = sb_step(i, tail0, acc0, True)
    tail, acc = lax.fori_loop(0, i, lambda jj, c: sb_step(i - 1 - jj, c[0], c[1], False), (tail, acc))
    for pp in range(SB_HEADS // 2):
        o = pair_out(acc[(2 * pp) * tq:(2 * pp + 1) * tq], acc[(2 * pp + 1) * tq:(2 * pp + 2) * tq], True)
        out_ref[:, pp * LANES:(pp + 1) * LANES] = o.astype(out_ref.dtype)

    q_i = qix[...].reshape(IDX_HEADS * tq, LANES)
    wfull = wix[...]
    wcol = [wfull[:, WIX_OFF + h:WIX_OFF + h + 1] for h in range(IDX_HEADS)]

    def score_step(j, diag):
        s = jnp.maximum(_dot_nt(q_i, kxr[blk(j), :]), 0.0)
        sc = s[0:tq] * wcol[0]
        for h in range(1, IDX_HEADS):
            sc = sc + s[h * tq:(h + 1) * tq] * wcol[h]
        if diag:
            sc = jnp.where(col <= row, sc, NEG_INF)
        u_scr[j] = _sortable(sc)

    def score_loop(j, c):
        score_step(j, False)
        return c

    lax.fori_loop(0, i, score_loop, 0)
    score_step(i, True)

    def count_ge(cand):
        cb = jnp.broadcast_to(cand, (tq, LANES))

        def inner(j, a):
            ub = u_scr[j]
            for c in range(tk // LANES):
                a = a + jnp.where(ub[:, c * LANES:(c + 1) * LANES] >= cb, 1.0, 0.0)
            return a

        a = lax.fori_loop(0, i + 1, inner, jnp.zeros((tq, LANES), F32))
        return jnp.sum(a, axis=1, keepdims=True)

    kk = float(kkeep)
    tau = _kth_largest(count_ge, kk, (tq, 1))
    need = kk - count_ge(tau + 1)

    q_d = qds[...].reshape(DSA_HEADS * tq, LANES)

    def dsa_step(j, m, l, acc, eqc, diag):
        ub = u_scr[j]
        eqf = jnp.where(ub == tau, 1.0, 0.0)
        pref = _dot(eqf.astype(BF16), tri_le) + eqc
        keep = (ub > tau) | ((ub == tau) & (pref <= need))
        if diag:
            keep = keep & (col <= row)
        bias = jnp.where(keep, 0.0, NEG_INF)
        kv = kdv[blk(j), :]
        lg = _dot_nt(q_d, kv) + jnp.concatenate([bias] * DSA_HEADS, axis=0)
        m_new = jnp.maximum(m, jnp.max(lg, axis=1, keepdims=True))
        m_safe = jnp.where(m_new == NEG_INF, 0.0, m_new)
        p = jnp.exp(lg - m_safe)
        alpha = jnp.exp(m - m_safe)
        l = alpha * l + jnp.sum(p, axis=1, keepdims=True)
        acc = alpha * acc + _dot(p.astype(BF16), kv)
        eqc = eqc + jnp.sum(eqf, axis=1, keepdims=True)
        return m_new, l, acc, eqc

    st = (jnp.full((DSA_HEADS * tq, 1), NEG_INF, F32), jnp.zeros((DSA_HEADS * tq, 1), F32),
          jnp.zeros((DSA_HEADS * tq, LANES), F32), jnp.zeros((tq, 1), F32))
    st = lax.fori_loop(0, i, lambda j, c: dsa_step(j, *c, False), st)
    _, l, acc, _ = dsa_step(i, *st, True)
    acc = acc / l
    for pp in range(DSA_HEADS // 2):
        o = pair_out(acc[(2 * pp) * tq:(2 * pp + 1) * tq], acc[(2 * pp + 1) * tq:(2 * pp + 2) * tq], True)
        c0 = SB_HEADS * HEAD_DIM + pp * LANES
        out_ref[:, c0:c0 + LANES] = o.astype(out_ref.dtype)

    q_r = qrp[...].reshape(MLA_HEADS * tq, LANES)
    q_n = [qnp[2 * pp:2 * pp + 2].reshape(2 * tq, LANES) for pp in range(MLA_HEADS // 2)]

    def mla_step(j, m, l, acc, diag):
        lg = _dot_nt(q_r, kxr[blk(j), :])
        nope = jnp.concatenate([_dot_nt(q_n[pp], knp[pp, blk(j), :]) for pp in range(MLA_HEADS // 2)], axis=0)
        lg = (lg + nope) * MLA_SCALE
        if diag:
            lg = jnp.where(stacked_mask(MLA_HEADS, False), lg, NEG_INF)
        m_new = jnp.maximum(m, jnp.max(lg, axis=1, keepdims=True))
        p = jnp.exp(lg - m_new).astype(BF16)
        alpha = jnp.exp(m - m_new)
        l = alpha * l + jnp.sum(p.astype(F32), axis=1, keepdims=True)
        pv = jnp.concatenate([_dot(p[h * tq:(h + 1) * tq], vml[h // 2, blk(j), :]) for h in range(MLA_HEADS)],
                             axis=0)
        return m_new, l, alpha * acc + pv

    st = (jnp.full((MLA_HEADS * tq, 1), NEG_INF, F32), jnp.zeros((MLA_HEADS * tq, 1), F32),
          jnp.zeros((MLA_HEADS * tq, LANES), F32))
    st = lax.fori_loop(0, i, lambda j, c: mla_step(j, *c, False), st)
    _, l, acc = mla_step(i, *st, True)
    acc = acc / l
    for pp in range(MLA_HEADS // 2):
        o = jnp.where(lane < HEAD_DIM, acc[(2 * pp) * tq:(2 * pp + 1) * tq], acc[(2 * pp + 1) * tq:(2 * pp + 2) * tq])
        c0 = (SB_HEADS + DSA_HEADS) * HEAD_DIM + pp * LANES
        out_ref[:, c0:c0 + LANES] = o.astype(out_ref.dtype)


def _prompt_attn(nb, seq, tq, kkeep, qsb, qds, qix, wix, qnp, qrp, ksv, kdv, kxr, knp, vml):
    nq = seq // tq

    def qh(nh):
        return pl.BlockSpec((nh, tq, LANES), lambda n, i: (0, n * nq + i, 0))

    kfull = pl.BlockSpec((seq, LANES), lambda n, i: (n, 0))
    kpair = pl.BlockSpec((MLA_HEADS // 2, seq, LANES), lambda n, i: (0, n, 0))
    return pl.pallas_call(
        functools.partial(_prompt_attn_kernel, tq, kkeep),
        out_shape=jax.ShapeDtypeStruct((nb * seq, D_MODEL), BF16),
        grid=(nb, nq),
        in_specs=[qh(SB_HEADS), qh(DSA_HEADS), qh(IDX_HEADS),
                  pl.BlockSpec((tq, LANES), lambda n, i: (n * nq + i, 0)),
                  qh(MLA_HEADS), qh(MLA_HEADS), kfull, kfull, kfull, kpair, kpair],
        out_specs=pl.BlockSpec((tq, D_MODEL), lambda n, i: (n * nq + i, 0)),
        scratch_shapes=[pltpu.VMEM((nq, tq, tq), I32)],
        compiler_params=_cparams(("arbitrary", "arbitrary")),
        name="prompt_attn",
    )(qsb, qds, qix, wix, qnp, qrp, ksv, kdv, kxr, knp, vml)


def _layer_norm(v, g, b):
    vc = v - jnp.mean(v, axis=-1, keepdims=True)
    var = jnp.mean(vc * vc, axis=-1, keepdims=True)
    return vc * lax.rsqrt(var + LN_EPS) * g + b


def _split3(x):
    a = x.astype(BF16)
    r = x - a.astype(F32)
    b = r.astype(BF16)
    c = (r - b.astype(F32)).astype(BF16)
    return a, b, c


def _post_attn_kernel(dn_alpha, x_ref, mix_ref, mod_ref, wout_ref, g1_ref, b1_ref, wr_ref, br_ref, cnt_in_ref,
                      x1_o, h2_o, idx_o, gate_o, rank_o, cnt_o, cnt_scr):
    i = pl.program_id(0)
    tm = x_ref.shape[0]

    @pl.when(i == 0)
    def _():
        cnt_scr[...] = cnt_in_ref[...]

    y = _dot(mix_ref[...], wout_ref[...])
    x1 = _layer_norm(dn_alpha * x_ref[...] + mod_ref[2] * y, g1_ref[...], b1_ref[...])
    x1_o[...] = x1
    h2 = x1 * (1.0 + mod_ref[4]) + mod_ref[3]
    h2_o[...] = h2

    ha, hb, hc = _split3(h2)
    wa, wb, wc = wr_ref[0], wr_ref[1], wr_ref[2]
    logits = (_dot(ha, wa) + (_dot(ha, wb) + _dot(hb, wa))
              + (_dot(ha, wc) + _dot(hb, wb) + _dot(hc, wa))) + br_ref[...]
    lane = lax.broadcasted_iota(I32, (tm, LANES), 1)
    work = jnp.where(lane < N_EXPERTS, logits, NEG_INF)
    sels, vals = [], []
    idx_acc = jnp.zeros((tm, LANES), I32)
    for k in range(TOP_K):
        mx = jnp.max(work, axis=1, keepdims=True)
        first = jnp.min(jnp.where(work == mx, lane, LANES), axis=1, keepdims=True)
        sel = lane == first
        sels.append(sel)
        vals.append(mx)
        idx_acc = jnp.where(lane == k, first, idx_acc)
        work = jnp.where(sel, NEG_INF, work)
    es = [jnp.exp(v - vals[0]) for v in vals]
    den = es[0] + es[1] + es[2] + es[3]
    gate_acc = jnp.zeros((tm, LANES), F32)
    for k in range(TOP_K):
        gate_acc = jnp.where(lane == k, es[k] / den, gate_acc)

    multi = jnp.zeros((tm, LANES), F32)
    for sel in sels:
        multi = multi + jnp.where(sel, 1.0, 0.0)
    r = lax.broadcasted_iota(I32, (tm, tm), 0)
    c = lax.broadcasted_iota(I32, (tm, tm), 1)
    before = jnp.where(c < r, 1.0, 0.0).astype(BF16)
    ahead = _dot(before, multi.astype(BF16)) + cnt_scr[...]
    rank_acc = jnp.zeros((tm, LANES), F32)
    for k in range(TOP_K):
        rk = jnp.sum(jnp.where(sels[k], ahead, 0.0), axis=1, keepdims=True)
        rank_acc = jnp.where(lane == k, rk, rank_acc)
    cnt_new = cnt_scr[...] + jnp.sum(multi, axis=0, keepdims=True)
    cnt_scr[...] = cnt_new
    idx_o[...] = idx_acc
    gate_o[...] = gate_acc
    rank_o[...] = rank_acc.astype(I32)
    cnt_o[...] = cnt_new


def _post_attn(is_prompt, dn_alpha, x2d, mix, mod, w_out_b, g1, b1, wr3, br, cnt_in, tm):
    t, d = x2d.shape
    nt = t // tm
    if is_prompt:
        tiles_per_b = nt // mod.shape[0]
        mod_spec = pl.BlockSpec((None, 6, 1, d), lambda i: (i // tiles_per_b, 0, 0, 0))
    else:
        mod_spec = pl.BlockSpec((6, tm, d), lambda i: (0, i, 0))
    row = lambda w: pl.BlockSpec((tm, w), lambda i: (i, 0))
    const2 = lambda s: pl.BlockSpec(s, lambda i: (0, 0))
    return pl.pallas_call(
        functools.partial(_post_attn_kernel, dn_alpha),
        out_shape=[jax.ShapeDtypeStruct((t, d), F32), jax.ShapeDtypeStruct((t, d), F32),
                   jax.ShapeDtypeStruct((t, LANES), I32), jax.ShapeDtypeStruct((t, LANES), F32),
                   jax.ShapeDtypeStruct((t, LANES), I32), jax.ShapeDtypeStruct((1, LANES), F32)],
        grid=(nt,),
        in_specs=[row(d), row(d), mod_spec, const2((d, d)), const2((1, d)), const2((1, d)),
                  pl.BlockSpec((3, d, LANES), lambda i: (0, 0, 0)), const2((1, LANES)), const2((1, LANES))],
        out_specs=[row(d), row(d), row(LANES), row(LANES), row(LANES), const2((1, LANES))],
        scratch_shapes=[pltpu.VMEM((1, LANES), F32)],
        compiler_params=_cparams(("arbitrary",)),
        name="post_attn_prompt" if is_prompt else "post_attn_sample",
    )(x2d, mix, mod, w_out_b, g1, b1, wr3, br, cnt_in)


def _dispatch_kernel(pos_ref, h2_ref, xs_in, xs_out, sem):
    del xs_in
    tm = h2_ref.shape[0]
    base = pl.program_id(0) * tm

    def row_copy(r, k):
        p = pos_ref[(base + r) * TOP_K + k]
        return pltpu.make_async_copy(h2_ref.at[pl.ds(r, 1)], xs_out.at[pl.ds(p, 1)], sem)

    def issue(r, c):
        for k in range(TOP_K):
            row_copy(r, k).start()
        return c

    def drain(r, c):
        for k in range(TOP_K):
            row_copy(r, k).wait()
        return c

    lax.fori_loop(0, tm, issue, 0)
    lax.fori_loop(0, tm, drain, 0)


def _dispatch(pos_flat, h2, xs, tm):
    t, d = h2.shape
    return pl.pallas_call(
        _dispatch_kernel,
        out_shape=jax.ShapeDtypeStruct(xs.shape, xs.dtype),
        grid_spec=pltpu.PrefetchScalarGridSpec(
            num_scalar_prefetch=1,
            grid=(t // tm,),
            in_specs=[pl.BlockSpec((tm, d), lambda i, pos: (i, 0)),
                      pl.BlockSpec(memory_space=pl.ANY)],
            out_specs=pl.BlockSpec(memory_space=pl.ANY),
            scratch_shapes=[pltpu.SemaphoreType.DMA(())],
        ),
        input_output_aliases={2: 0},
        compiler_params=_cparams(("arbitrary",)),
        name="moe_dispatch",
    )(pos_flat, h2, xs)


def _moe_gemm_kernel(te_ref, nv_ref, xs_ref, wgu_ref, bgu_ref, wd_ref, bd_ref, ys_ref, wgu_b, wd_b):
    i = pl.program_id(0)

    @pl.when(i < nv_ref[0])
    def _():
        prev = te_ref[jnp.maximum(i - 1, 0)]

        @pl.when((i == 0) | (te_ref[i] != prev))
        def _():
            wgu_b[...] = wgu_ref[...].astype(BF16)
            wd_b[...] = wd_ref[...].astype(BF16)

        gu = _dot(xs_ref[...].astype(BF16), wgu_b[...]) + bgu_ref[...]
        g = jnp.minimum(gu[:, :D_FF], SWIGLU_LIMIT)
        u = jnp.clip(gu[:, D_FF:], -SWIGLU_LIMIT, SWIGLU_LIMIT)
        act = (u + 1.0) * g * (1.0 / (1.0 + jnp.exp(-SWIGLU_ALPHA * g)))
        ys_ref[...] = _dot(act.astype(BF16), wd_b[...]) + bd_ref[...]

    @pl.when(i >= nv_ref[0])
    def _():
        ys_ref[...] = jnp.zeros_like(ys_ref)


def _moe_gemm(layer, tile_expert, n_valid, xs, w_gate_up, b_gate_up, w_down, b_down, tmg):
    p, d = xs.shape
    nt = p // tmg
    f2 = w_gate_up.shape[-1]

    def tile(i, te, nv):
        return (jnp.minimum(i, nv[0] - 1), 0)

    def wmap(i, te, nv):
        return (layer, te[jnp.minimum(i, nv[0] - 1)], 0, 0)

    return pl.pallas_call(
        _moe_gemm_kernel,
        out_shape=jax.ShapeDtypeStruct((p, d), F32),
        grid_spec=pltpu.PrefetchScalarGridSpec(
            num_scalar_prefetch=2,
            grid=(nt,),
            in_specs=[pl.BlockSpec((tmg, d), tile),
                      pl.BlockSpec((None, None, d, f2), wmap),
                      pl.BlockSpec((None, None, 1, f2), wmap),
                      pl.BlockSpec((None, None, D_FF, d), wmap),
                      pl.BlockSpec((None, None, 1, d), wmap)],
            out_specs=pl.BlockSpec((tmg, d), lambda i, te, nv: (i, 0)),
            scratch_shapes=[pltpu.VMEM((d, f2), BF16), pltpu.VMEM((D_FF, d), BF16)],
        ),
        compiler_params=_cparams(("arbitrary",)),
        name="moe_gemm",
    )(tile_expert, n_valid, xs, w_gate_up, b_gate_up.reshape(b_gate_up.shape[:2] + (1, f2)),
      w_down, b_down.reshape(b_down.shape[:2] + (1, d)))


def _combine_kernel(dn_alpha, pos_ref, x1_ref, gate_ref, mod_ref, g2_ref, b2_ref, ys_ref, out_ref, buf, sem):
    tm = x1_ref.shape[0]
    base = pl.program_id(0) * tm

    def row_copy(r, k):
        p = pos_ref[(base + r) * TOP_K + k]
        return pltpu.make_async_copy(ys_ref.at[pl.ds(p, 1)], buf.at[k, pl.ds(r, 1)], sem)

    def issue(r, c):
        for k in range(TOP_K):
            row_copy(r, k).start()
        return c

    def drain(r, c):
        for k in range(TOP_K):
            row_copy(r, k).wait()
        return c

    lax.fori_loop(0, tm, issue, 0)
    lax.fori_loop(0, tm, drain, 0)
    gate = gate_ref[...]
    ffn = gate[:, 0:1] * buf[0]
    for k in range(1, TOP_K):
        ffn = ffn + gate[:, k:k + 1] * buf[k]
    out_ref[...] = _layer_norm(dn_alpha * x1_ref[...] + mod_ref[5] * ffn, g2_ref[...], b2_ref[...])


def _combine(is_prompt, dn_alpha, pos_flat, x1, gate, mod, g2, b2, ys, tm):
    t, d = x1.shape
    nt = t // tm
    if is_prompt:
        tiles_per_b = nt // mod.shape[0]
        mod_spec = pl.BlockSpec((None, 6, 1, d), lambda i, pos: (i // tiles_per_b, 0, 0, 0))
    else:
        mod_spec = pl.BlockSpec((6, tm, d), lambda i, pos: (0, i, 0))
    return pl.pallas_call(
        functools.partial(_combine_kernel, dn_alpha),
        out_shape=jax.ShapeDtypeStruct((t, d), F32),
        grid_spec=pltpu.PrefetchScalarGridSpec(
            num_scalar_prefetch=1,
            grid=(nt,),
            in_specs=[pl.BlockSpec((tm, d), lambda i, pos: (i, 0)),
                      pl.BlockSpec((tm, LANES), lambda i, pos: (i, 0)),
                      mod_spec,
                      pl.BlockSpec((1, d), lambda i, pos: (0, 0)),
                      pl.BlockSpec((1, d), lambda i, pos: (0, 0)),
                      pl.BlockSpec(memory_space=pl.ANY)],
            out_specs=pl.BlockSpec((tm, d), lambda i, pos: (i, 0)),
            scratch_shapes=[pltpu.VMEM((TOP_K, tm, d), F32), pltpu.SemaphoreType.DMA(())],
        ),
        compiler_params=_cparams(("arbitrary",)),
        name="moe_combine",
    )(pos_flat, x1, gate, mod, g2, b2, ys)


SUB = 256


def _paged_stream(pt_ref, pools, bufs, sem, layer, cp, descending):
    b = pl.program_id(0)
    c = pl.program_id(1)
    nb = pl.num_programs(0)
    nc = pl.num_programs(1)
    step = b * nc + c
    slot = lax.rem(step, 2)

    def copies(bb, cc, sl):
        page0 = ((nc - 1 - cc) if descending else cc) * cp
        out = []
        for r, (pool, buf) in enumerate(zip(pools, bufs)):
            for p in range(cp):
                phys = pt_ref[bb, page0 + p]
                out.append(pltpu.make_async_copy(pool.at[layer, phys],
                                                 buf.at[sl, pl.ds(p * PAGE_SIZE, PAGE_SIZE)],
                                                 sem.at[r, sl]))
        return out

    @pl.when(step == 0)
    def _():
        for cpy in copies(b, c, slot):
            cpy.start()

    @pl.when(step + 1 < nb * nc)
    def _():
        nxt = step + 1
        for cpy in copies(nxt // nc, lax.rem(nxt, nc), 1 - slot):
            cpy.start()

    for cpy in copies(b, c, slot):
        cpy.wait()
    return slot


def _paged_call(kernel_fn, name, page_table, pools, dense_in, dense_specs, out_shape, out_specs,
                scratch, cp):
    db, n_pages = page_table.shape
    nc = n_pages // cp
    ck = cp * PAGE_SIZE
    bufs = [pltpu.VMEM((2, ck, pool.shape[-1]), pool.dtype) for pool in pools]
    return pl.pallas_call(
        kernel_fn,
        out_shape=out_shape,
        grid_spec=pltpu.PrefetchScalarGridSpec(
            num_scalar_prefetch=1,
            grid=(db, nc),
            in_specs=[pl.BlockSpec(memory_space=pl.ANY)] * len(pools) + dense_specs,
            out_specs=out_specs,
            scratch_shapes=bufs + [pltpu.SemaphoreType.DMA((len(pools), 2))] + scratch,
        ),
        compiler_params=_cparams(("arbitrary", "arbitrary")),
        name=name,
    )(page_table, *pools, *dense_in)


def _tri(n, strict_lower_rows):
    r = lax.broadcasted_iota(I32, (n, n), 0)
    c = lax.broadcasted_iota(I32, (n, n), 1)
    return jnp.where((r > c) if strict_lower_rows else (r <= c), 1.0, 0.0).astype(BF16)


def _sample_sb_kernel(layer, cp, pt_ref, kpool, vpool, q_ref, o_ref, kbuf, vbuf, sem,
                      z_scr, e_scr, tail_scr, acc_scr):
    c = pl.program_id(1)
    nsb = cp * PAGE_SIZE // SUB
    nh = SB_HEADS
    slot = _paged_stream(pt_ref, (kpool, vpool), (kbuf, vbuf), sem, layer, cp, True)

    @pl.when(c == 0)
    def _():
        tail_scr[...] = jnp.zeros_like(tail_scr)
        acc_scr[...] = jnp.zeros_like(acc_scr)

    q = q_ref[...][:, :HEAD_DIM].astype(BF16)
    for j in range(nsb):
        kj = kbuf[slot, pl.ds(j * SUB, SUB), :].astype(BF16)
        z_scr[pl.ds(nh * j, nh), :] = _dot_nt(q, kj)
    z = z_scr[...]
    lg = jnp.log1p(jnp.exp(-jnp.abs(z)))
    lsig = jnp.minimum(z, 0.0) - lg
    l1m = -jnp.maximum(z, 0.0) - lg
    hi = l1m.astype(BF16)
    lo = (l1m - hi.astype(F32)).astype(BF16)
    tri = _tri(SUB, True)
    tin = _dot(hi, tri) + _dot(lo, tri)
    tot = jnp.sum(l1m, axis=1, keepdims=True)
    run = tail_scr[...]
    pieces = [None] * nsb
    for j in reversed(range(nsb)):
        pieces[j] = run
        run = run + tot[nh * j:nh * (j + 1)]
    tail_scr[...] = run
    e_scr[...] = jnp.exp(lsig + tin + jnp.concatenate(pieces, axis=0))
    acc = acc_scr[...]
    for j in range(nsb):
        vj = vbuf[slot, pl.ds(j * SUB, SUB), :].astype(BF16)
        acc = acc + _dot(e_scr[pl.ds(nh * j, nh), :].astype(BF16), vj)
    acc_scr[...] = acc

    @pl.when(c == pl.num_programs(1) - 1)
    def _():
        o_ref[...] = acc


def _sample_sb(layer, cp, page_table, kpool, vpool, q):
    db = page_table.shape[0]
    nsb = cp * PAGE_SIZE // SUB
    return _paged_call(
        functools.partial(_sample_sb_kernel, layer, cp), "sample_sb", page_table, [kpool, vpool], [q],
        [pl.BlockSpec((None, SB_HEADS, LANES), lambda b, c, pt: (b, 0, 0))],
        jax.ShapeDtypeStruct((db, SB_HEADS, HEAD_DIM), F32),
        pl.BlockSpec((None, SB_HEADS, HEAD_DIM), lambda b, c, pt: (b, 0, 0)),
        [pltpu.VMEM((SB_HEADS * nsb, SUB), F32), pltpu.VMEM((SB_HEADS * nsb, SUB), F32),
         pltpu.VMEM((SB_HEADS, 1), F32), pltpu.VMEM((SB_HEADS, HEAD_DIM), F32)], cp)


def _sample_idx_kernel(layer, cp, pt_ref, kpool, q_ref, w_ref, knew_ref, o_ref, kbuf, sem):
    c = pl.program_id(1)
    nc = pl.num_programs(1)
    nsb = cp * PAGE_SIZE // SUB
    slot = _paged_stream(pt_ref, (kpool,), (kbuf,), sem, layer, cp, False)
    q = q_ref[...][:, :IDX_DIM].astype(BF16)
    w = w_ref[...]
    for j in range(nsb):
        kj = kbuf[slot, pl.ds(j * SUB, SUB), :].astype(BF16)
        s = jnp.maximum(_dot_nt(q, kj), 0.0) * w
        o_ref[pl.ds(c * nsb + j, 1), :] = jnp.sum(s, axis=0, keepdims=True)

    @pl.when(c == nc - 1)
    def _():
        kn = knew_ref[...].astype(BF16).astype(F32)
        s = jnp.maximum(jnp.sum(q.astype(F32) * kn, axis=1, keepdims=True), 0.0) * w
        sc = jnp.sum(s, axis=0, keepdims=True)
        lane = lax.broadcasted_iota(I32, (1, SUB), 1)
        o_ref[pl.ds(nc * nsb, 1), :] = jnp.where(lane == 0, sc, NEG_INF)


def _sample_idx(layer, cp, page_table, kpool, q, w8, knew):
    db, n_pages = page_table.shape
    nrow = n_pages * PAGE_SIZE // SUB + 1
    per_b = lambda s: pl.BlockSpec((None,) + s, lambda b, c, pt: (b, 0, 0))
    return _paged_call(
        functools.partial(_sample_idx_kernel, layer, cp), "sample_idx", page_table, [kpool], [q, w8, knew],
        [per_b((IDX_HEADS, LANES)), per_b((IDX_HEADS, 1)), per_b((1, IDX_DIM))],
        jax.ShapeDtypeStruct((db, nrow, SUB), F32), per_b((nrow, SUB)), [], cp)


def _topk_thr_kernel(kkeep, s_ref, tau_ref, need_ref, u_scr):
    nrow, db, sub = s_ref.shape

    def fill(j, c):
        u_scr[j] = _sortable(s_ref[j])
        return c

    lax.fori_loop(0, nrow, fill, 0)

    def count_ge(cand):
        cb = jnp.broadcast_to(cand, (db, LANES))

        def inner(j, a):
            ub = u_scr[j]
            for cc in range(sub // LANES):
                a = a + jnp.where(ub[:, cc * LANES:(cc + 1) * LANES] >= cb, 1.0, 0.0)
            return a

        a = lax.fori_loop(0, nrow, inner, jnp.zeros((db, LANES), F32))
        return jnp.sum(a, axis=1, keepdims=True)

    kk = float(kkeep)
    tau = _kth_largest(count_ge, kk, (db, 1))
    tau_ref[...] = tau
    need_ref[...] = kk - count_ge(tau + 1)


def _topk_thr(scores_t, kkeep):
    nrow, db, sub = scores_t.shape
    return pl.pallas_call(
        functools.partial(_topk_thr_kernel, kkeep),
        out_shape=[jax.ShapeDtypeStruct((db, 1), I32), jax.ShapeDtypeStruct((db, 1), F32)],
        scratch_shapes=[pltpu.VMEM((nrow, db, sub), I32)],
        compiler_params=pltpu.CompilerParams(vmem_limit_bytes=VMEM_LIMIT),
        name="topk_threshold",
    )(scores_t)


def _sample_dsa_kernel(layer, cp, pt_ref, kpool, vpool, q_ref, sc_ref, tau_ref, need_ref, knew_ref, vnew_ref,
                       o_ref, kbuf, vbuf, sem, z_scr, p_scr, m_scr, l_scr, acc_scr, eqc_scr):
    c = pl.program_id(1)
    nc = pl.num_programs(1)
    nsb = cp * PAGE_SIZE // SUB
    nh = DSA_HEADS
    slot = _paged_stream(pt_ref, (kpool, vpool), (kbuf, vbuf), sem, layer, cp, False)

    @pl.when(c == 0)
    def _():
        m_scr[...] = jnp.full_like(m_scr, NEG_INF)
        l_scr[...] = jnp.zeros_like(l_scr)
        acc_scr[...] = jnp.zeros_like(acc_scr)
        eqc_scr[...] = jnp.zeros_like(eqc_scr)

    tau = tau_ref[...]
    need = need_ref[...]
    q = q_ref[...][:, :HEAD_DIM].astype(BF16)
    for j in range(nsb):
        kj = kbuf[slot, pl.ds(j * SUB, SUB), :].astype(BF16)
        z_scr[pl.ds(nh * j, nh), :] = _dot_nt(q, kj)
    ub = _sortable(sc_ref[pl.ds(c * nsb, nsb), :])
    eqf = jnp.where(ub == tau, 1.0, 0.0)
    pref = _dot(eqf.astype(BF16), _tri(SUB, False))
    rowtot = jnp.sum(eqf, axis=1, keepdims=True)
    run = eqc_scr[...]
    offs = []
    for j in range(nsb):
        offs.append(run)
        run = run + rowtot[j:j + 1]
    eqc_scr[...] = run
    pref = pref + jnp.concatenate(offs, axis=0)
    keep = (ub > tau) | ((ub == tau) & (pref <= need))
    bias = jnp.where(keep, 0.0, NEG_INF)
    bias = jnp.concatenate([jnp.broadcast_to(bias[j:j + 1], (nh, SUB)) for j in range(nsb)], axis=0)
    lg = z_scr[...] + bias
    m_old = m_scr[...]
    m_new = jnp.maximum(m_old, jnp.max(lg, axis=1, keepdims=True))
    m_safe = jnp.where(m_new == NEG_INF, 0.0, m_new)
    p_scr[...] = jnp.exp(lg - m_safe)
    alpha = jnp.exp(m_old - m_safe)
    l_scr[...] = alpha * l_scr[...] + jnp.sum(p_scr[...], axis=1, keepdims=True)
    m_scr[...] = m_new
    for j in range(nsb):
        vj = vbuf[slot, pl.ds(j * SUB, SUB), :].astype(BF16)
        rows = pl.ds(nh * j, nh)
        acc_scr[rows, :] = alpha[nh * j:nh * (j + 1)] * acc_scr[rows, :] + _dot(p_scr[rows, :].astype(BF16), vj)

    @pl.when(c == nc - 1)
    def _():
        qf = q.astype(F32)
        kn = knew_ref[...].astype(BF16).astype(F32)
        vn = vnew_ref[...].astype(BF16).astype(F32)
        lg_new = jnp.sum(qf * kn, axis=1, keepdims=True)
        u_new = _sortable(sc_ref[pl.ds(nc * nsb, 1), 0:1])
        keep_new = (u_new > tau) | ((u_new == tau) & (eqc_scr[...] + 1.0 <= need))
        lg_new = jnp.where(keep_new, lg_new, NEG_INF)
        ms = [m_scr[pl.ds(nh * j, nh), :] for j in range(nsb)]
        mx = lg_new
        for mj in ms:
            mx = jnp.maximum(mx, mj)
        w_new = jnp.exp(lg_new - mx)
        den = w_new
        num = w_new.astype(BF16).astype(F32) * vn
        for j, mj in enumerate(ms):
            wj = jnp.exp(mj - mx)
            den = den + wj * l_scr[pl.ds(nh * j, nh), :]
            num = num + wj * acc_scr[pl.ds(nh * j, nh), :]
        o_ref[...] = num / den


def _sample_dsa(layer, cp, page_table, kpool, vpool, q, scores, tau, need, knew, vnew):
    db, n_pages = page_table.shape
    nsb = cp * PAGE_SIZE // SUB
    nrow = scores.shape[1]
    per_b = lambda s: pl.BlockSpec((None,) + s, lambda b, c, pt: (b, 0, 0))
    rows = DSA_HEADS * nsb
    return _paged_call(
        functools.partial(_sample_dsa_kernel, layer, cp), "sample_dsa", page_table, [kpool, vpool],
        [q, scores, tau, need, knew, vnew],
        [per_b((DSA_HEADS, LANES)), per_b((nrow, SUB)), per_b((1, 1)), per_b((1, 1)),
         per_b((1, HEAD_DIM)), per_b((1, HEAD_DIM))],
        jax.ShapeDtypeStruct((db, DSA_HEADS, HEAD_DIM), F32), per_b((DSA_HEADS, HEAD_DIM)),
        [pltpu.VMEM((rows, SUB), F32), pltpu.VMEM((rows, SUB), F32), pltpu.VMEM((rows, 1), F32),
         pltpu.VMEM((rows, 1), F32), pltpu.VMEM((rows, HEAD_DIM), F32), pltpu.VMEM((1, 1), F32)], cp)


def _sample_mla_kernel(layer, cp, pt_ref, cpool, rpool, ql_ref, qr_ref, cnew_ref, rnew_ref, o_ref,
                       cbuf, rbuf, sem, cb_scr, z_scr, p_scr, m_scr, l_scr, acc_scr):
    c = pl.program_id(1)
    nc = pl.num_programs(1)
    nsb = cp * PAGE_SIZE // SUB
    nh = MLA_HEADS
    slot = _paged_stream(pt_ref, (cpool, rpool), (cbuf, rbuf), sem, layer, cp, False)

    @pl.when(c == 0)
    def _():
        m_scr[...] = jnp.full_like(m_scr, NEG_INF)
        l_scr[...] = jnp.zeros_like(l_scr)
        acc_scr[...] = jnp.zeros_like(acc_scr)

    ql = ql_ref[...].astype(BF16)
    qr = qr_ref[...].astype(BF16)
    for j in range(nsb):
        keys = pl.ds(j * SUB, SUB)
        cj = cbuf[slot, keys, :].astype(BF16)
        cb_scr[keys, :] = cj
        rj = rbuf[slot, keys, :].astype(BF16)
        z_scr[pl.ds(nh * j, nh), :] = (_dot_nt(ql, cj) + _dot_nt(qr, rj)) * MLA_SCALE
    lg = z_scr[...]
    m_old = m_scr[...]
    m_new = jnp.maximum(m_old, jnp.max(lg, axis=1, keepdims=True))
    p_scr[...] = jnp.exp(lg - m_new)
    alpha = jnp.exp(m_old - m_new)
    l_scr[...] = alpha * l_scr[...] + jnp.sum(p_scr[...].astype(BF16).astype(F32), axis=1, keepdims=True)
    m_scr[...] = m_new
    for j in range(nsb):
        rows = pl.ds(nh * j, nh)
        pv = _dot(p_scr[rows, :].astype(BF16), cb_scr[pl.ds(j * SUB, SUB), :])
        acc_scr[rows, :] = alpha[nh * j:nh * (j + 1)] * acc_scr[rows, :] + pv

    @pl.when(c == nc - 1)
    def _():
        cn = cnew_ref[...].astype(BF16).astype(F32)
        rn = rnew_ref[...].astype(BF16).astype(F32)
        lg_new = (jnp.sum(ql.astype(F32) * cn, axis=1, keepdims=True)
                  + jnp.sum(qr.astype(F32) * rn, axis=1, keepdims=True)) * MLA_SCALE
        ms = [m_scr[pl.ds(nh * j, nh), :] for j in range(nsb)]
        mx = lg_new
        for mj in ms:
            mx = jnp.maximum(mx, mj)
        w_new = jnp.exp(lg_new - mx).astype(BF16).astype(F32)
        den = w_new
        num = w_new * cn
        for j, mj in enumerate(ms):
            wj = jnp.exp(mj - mx)
            den = den + wj * l_scr[pl.ds(nh * j, nh), :]
            num = num + wj * acc_scr[pl.ds(nh * j, nh), :]
        o_ref[...] = num / den


def _sample_mla(layer, cp, page_table, cpool, rpool, ql, qr, cnew, rnew):
    db = page_table.shape[0]
    nsb = cp * PAGE_SIZE // SUB
    rows = MLA_HEADS * nsb
    per_b = lambda s: pl.BlockSpec((None,) + s, lambda b, c, pt: (b, 0, 0))
    return _paged_call(
        functools.partial(_sample_mla_kernel, layer, cp), "sample_mla", page_table, [cpool, rpool],
        [ql, qr, cnew, rnew],
        [per_b((MLA_HEADS, KV_LORA)), per_b((MLA_HEADS, MLA_ROPE)), per_b((1, KV_LORA)), per_b((1, MLA_ROPE))],
        jax.ShapeDtypeStruct((db, MLA_HEADS, KV_LORA), F32), per_b((MLA_HEADS, KV_LORA)),
        [pltpu.VMEM((cp * PAGE_SIZE, KV_LORA), BF16), pltpu.VMEM((rows, SUB), F32), pltpu.VMEM((rows, SUB), F32),
         pltpu.VMEM((rows, 1), F32), pltpu.VMEM((rows, 1), F32), pltpu.VMEM((rows, KV_LORA), F32)], cp)


def _mla_uv_kernel(olat_ref, wuv_ref, o_ref):
    for h in range(MLA_HEADS):
        o_ref[:, h * HEAD_DIM:(h + 1) * HEAD_DIM] = _dot(olat_ref[h].astype(BF16), wuv_ref[h])


def _mla_uv(olat_t, wuv_t):
    db = olat_t.shape[1]
    return pl.pallas_call(
        _mla_uv_kernel,
        out_shape=jax.ShapeDtypeStruct((db, MLA_HEADS * HEAD_DIM), F32),
        name="mla_uv",
    )(olat_t, wuv_t)


def _largest_divisor(n, cap):
    t = cap
    while n % t:
        t //= 2
    return t


def _split3_host(w):
    a = w.astype(BF16)
    r = w - a.astype(F32)
    b = r.astype(BF16)
    c = (r - b.astype(F32)).astype(BF16)
    return jnp.stack([a, b, c], axis=0)


def kernel(x_prompt, x_sample, cache_sb_k, cache_sb_v, cache_dsa_k, cache_dsa_v, cache_dsa_idx_k,
           cache_mla_ckv, cache_mla_krope, page_table, c_prompt, c_sample,
           w_ada, b_ada, w_in, g_q_norm, g_kv_norm, w_uq, w_uk, w_uv, w_out,
           ln1_g, ln1_b, ln2_g, ln2_b, w_router, b_router, w_gate_up, b_gate_up, w_down, b_down):
    nb, seq, d = x_prompt.shape
    db, dec_seq, _ = x_sample.shape
    assert dec_seq == 1 and d == D_MODEL and nb <= 8
    depth = w_in.shape[0]
    n_pages = page_table.shape[1]
    past = n_pages * PAGE_SIZE
    dn_alpha = (2 * depth) ** 0.25
    tp, ts = nb * seq, db

    tm_p = _largest_divisor(seq, 512)
    tq = _largest_divisor(seq, 256)
    cp = _largest_divisor(n_pages, 16)
    tmg = 512
    kkeep_p = min(INDEX_TOPK, seq // 4)
    kkeep_s = min(INDEX_TOPK, (past + dec_seq) // 4)
    assert kkeep_p <= tq and kkeep_s <= SUB and (cp * PAGE_SIZE) % SUB == 0

    c_all = jnp.concatenate([c_prompt, jnp.zeros((8 - nb, d), F32), c_sample], axis=0)
    mod_all = _modulation(c_all, w_ada, b_ada)
    tab_p = _rope_tables(jnp.arange(seq, dtype=I32))
    tab_s = _rope_tables(jnp.full((1,), past, I32))

    n_tiles = -(-(TOP_K * (tp + ts)) // tmg) + N_EXPERTS
    xp = x_prompt.reshape(tp, d)
    xs = x_sample.reshape(ts, d)
    rows_p, rows_s = [], []
    for l in range(depth):
        mod_p = mod_all[l, :nb].reshape(nb, 6, 1, d)
        mod_s = mod_all[l, 8:8 + db].reshape(db, 6, d).transpose(1, 0, 2)
        w_in_p = _prep_w_in(w_in[l])
        w_uq_p = _prep_w_uq(w_uq[l])
        g_q, g_kv = g_q_norm[l][None], g_kv_norm[l][None]
        w_uk2 = w_uk[l].reshape(KV_LORA, MLA_HEADS * MLA_NOPE).astype(BF16)
        w_uv2 = w_uv[l].reshape(KV_LORA, MLA_HEADS * HEAD_DIM).astype(BF16)
        ukt = w_uk[l].transpose(1, 2, 0)
        z = jnp.zeros_like(ukt)
        even = (jnp.arange(MLA_HEADS) % 2 == 0)[:, None, None]
        w_ukt = jnp.where(even, jnp.concatenate([ukt, z], 1), jnp.concatenate([z, ukt], 1)).astype(BF16)

        (ksb, vsb, kds, vds, kix, ckv, kr, qsb, qds, qix, wix, qnp, qrp, ksv, kdv, kxr, knp, vml) = _in_proj(
            True, xp, mod_p, tab_p, w_in_p, g_q, g_kv, w_uq_p, w_uk2, w_uv2, tm_p, BF16)
        rows_p.append([a.reshape(nb, seq, -1) for a in (ksb, vsb, kds, vds, kix, ckv, kr)])
        mix_p = _prompt_attn(nb, seq, tq, kkeep_p, qsb, qds, qix, wix, qnp, qrp, ksv, kdv, kxr, knp, vml)

        (ksb, vsb, kds, vds, kix, ckv, kr, qsb, qds, qix, wix, qlat, qrp, _, _, _) = _in_proj(
            False, xs, mod_s, tab_s, w_in_p, g_q, g_kv, w_uq_p, w_ukt, w_uv2, ts, F32)
        rows_s.append([a.reshape(db, 1, -1) for a in (ksb, vsb, kds, vds, kix, ckv, kr)])
        hm = lambda a: a.transpose(1, 0, 2)
        o_sb = _sample_sb(l, cp, page_table, cache_sb_k, cache_sb_v, hm(qsb))
        w8 = wix[:, WIX_OFF:WIX_OFF + IDX_HEADS][:, :, None]
        scores = _sample_idx(l, cp, page_table, cache_dsa_idx_k, hm(qix), w8, kix[:, None, :])
        tau, need = _topk_thr(hm(scores), kkeep_s)
        o_ds = _sample_dsa(l, cp, page_table, cache_dsa_k, cache_dsa_v, hm(qds), scores,
                           tau[:, :, None], need[:, :, None], kds[:, None, :], vds[:, None, :])
        o_lat = _sample_mla(l, cp, page_table, cache_mla_ckv, cache_mla_krope, hm(qlat),
                            hm(qrp[:, :, KR_OFF:KR_OFF + MLA_ROPE]), ckv[:, None, :], kr[:, None, :])
        o_mla = _mla_uv(hm(o_lat), w_uv[l].transpose(1, 0, 2).astype(BF16))
        mix_s = jnp.concatenate([o_sb.reshape(db, -1), o_ds.reshape(db, -1), o_mla], axis=1).astype(BF16)

        w_out_b = w_out[l].astype(BF16)
        wr3 = _split3_host(jnp.pad(w_router[l], ((0, 0), (0, LANES - N_EXPERTS))))
        br = jnp.pad(b_router[l], (0, LANES - N_EXPERTS))[None]
        g1, b1, g2, b2 = ln1_g[l][None], ln1_b[l][None], ln2_g[l][None], ln2_b[l][None]
        x1_p, h2_p, idx_p, gate_p, rank_p, cnt_p = _post_attn(
            True, dn_alpha, xp, mix_p, mod_p, w_out_b, g1, b1, wr3, br, jnp.zeros((1, LANES), F32), tm_p)
        x1_s, h2_s, idx_s, gate_s, rank_s, cnt = _post_attn(
            False, dn_alpha, xs, mix_s, mod_s, w_out_b, g1, b1, wr3, br, cnt_p, ts)

        counts = cnt[0, :N_EXPERTS].astype(I32)
        padded = (counts + tmg - 1) // tmg * tmg
        ends = jnp.cumsum(padded)
        starts = ends - padded
        pos_p = (starts[idx_p[:, :TOP_K]] + rank_p[:, :TOP_K]).reshape(-1)
        pos_s = (starts[idx_s[:, :TOP_K]] + rank_s[:, :TOP_K]).reshape(-1)
        n_valid = (ends[-1:] // tmg).astype(I32)
        tile_expert = jnp.minimum(
            jnp.searchsorted(ends, jnp.arange(n_tiles, dtype=I32) * tmg, side="right"), N_EXPERTS - 1).astype(I32)

        slots = jnp.zeros((n_tiles * tmg, d), F32)
        slots = _dispatch(pos_p, h2_p, slots, tm_p)
        slots = _dispatch(pos_s, h2_s, slots, ts)
        ys = _moe_gemm(l, tile_expert, n_valid, slots, w_gate_up, b_gate_up, w_down, b_down, tmg)
        xp = _combine(True, dn_alpha, pos_p, x1_p, gate_p, mod_p, g2, b2, ys, tm_p)
        xs = _combine(False, dn_alpha, pos_s, x1_s, gate_s, mod_s, g2, b2, ys, ts)

    def stk(rows, j):
        return jnp.stack([r[j] for r in rows], axis=0)

    return ((xp.reshape(nb, seq, d), xs.reshape(db, 1, d))
            + tuple(stk(rows_p, j) for j in range(7)) + tuple(stk(rows_s, j) for j in range(7)))
```
